```python
import jax, jax.numpy as jnp
from jax import lax
import numpy as np

D_MODEL = 2048
BATCH = 1
SEQ = 8192
DEPTH = 1
DEC_BATCH = 128
DEC_SEQ = 1
PAST_LEN = 2048
PAGE_SIZE = 128

POOL_WINDOWS = (2, 4, 8, 16)
POOL_GROUPS = 4
POOL_GROUP = 256
POOL_WIDTH = POOL_GROUPS * POOL_GROUP
POOL_BUF = max(POOL_WINDOWS) - 1
N_HEADS = 8
HEAD_DIM = 128
ATTN_WIDTH = N_HEADS * HEAD_DIM
IDX_HEADS = 16
IDX_DIM = 64
TOPK_MAX = 256
QBLOCK = 128
ROPE_THETA = 10000.0
N_KEYS = 128
N_EXPERTS = N_KEYS * N_KEYS
PEER_HEADS = 8
PEER_QDIM = 256
PEER_HALF = PEER_QDIM // 2
PEER_TOPK = 16
PEER_BLOCK = 128
EPS = 1e-6
IN_SIZES = (POOL_WIDTH, ATTN_WIDTH, ATTN_WIDTH, ATTN_WIDTH, IDX_HEADS * IDX_DIM, IDX_DIM, IDX_HEADS, D_MODEL, D_MODEL)
IN_WIDTH = sum(IN_SIZES)

kernel_name = 'dsa_pool_peer_hybrid_step'


def rmsnorm(x, g):
    xf = x.astype(jnp.float32)
    var = jnp.mean(xf * xf, axis=-1, keepdims=True)
    return (xf * lax.rsqrt(var + EPS) * g.astype(jnp.float32)).astype(x.dtype)


def rope(x, pos):
    half = x.shape[-1] // 2
    freqs = ROPE_THETA ** (-jnp.arange(half, dtype=jnp.float32) / half)
    ang = pos.astype(jnp.float32)[:, None] * freqs[None, :]
    ang = ang.reshape((ang.shape[0],) + (1,) * (x.ndim - 3) + (half,))
    cos, sin = jnp.cos(ang), jnp.sin(ang)
    xf = x.astype(jnp.float32)
    x1, x2 = xf[..., :half], xf[..., half:]
    return jnp.concatenate([x1 * cos - x2 * sin, x2 * cos + x1 * sin], axis=-1).astype(x.dtype)


def project_inputs(h, w_in, pos):
    B, T, _ = h.shape
    p = h @ w_in
    parts = []
    o = 0
    for s in IN_SIZES:
        parts.append(p[..., o:o + s])
        o += s
    u, q, k, v, qi, ki, wi, ga, gb = parts
    q = rope(q.reshape(B, T, N_HEADS, HEAD_DIM), pos)
    k = rope(k.reshape(B, T, N_HEADS, HEAD_DIM), pos)
    v = v.reshape(B, T, N_HEADS, HEAD_DIM)
    qi = rope(qi.reshape(B, T, IDX_HEADS, IDX_DIM), pos)
    ki = rope(ki, pos)
    return u, q, k, v, qi, ki, wi, ga, gb


def pool_mix(ctx, pos, pool_w, pool_scale):
    B, Lc, C = ctx.shape
    T = pos.shape[0]
    P = Lc - T
    cf = ctx.astype(jnp.float32)
    csum = jnp.cumsum(jnp.concatenate([jnp.zeros((B, 1, C), jnp.float32), cf], axis=1), axis=1)
    tok = cf[:, P:]
    outs = []
    for g, w in enumerate(POOL_WINDOWS):
        lo, hi = g * POOL_GROUP, (g + 1) * POOL_GROUP
        wsum = csum[:, P + 1:, lo:hi] - csum[:, P + 1 - w:Lc + 1 - w, lo:hi]
        cnt = jnp.minimum(w, pos + 1).astype(jnp.float32)[None, :, None]
        outs.append(wsum / cnt - tok[:, :, lo:hi])
    pooled = jnp.stack(outs, axis=2)
    mixed = jnp.einsum('btgc,gcd->btgd', pooled, pool_w.astype(jnp.float32)).reshape(B, T, C)
    return (mixed * pool_scale.astype(jnp.float32)).astype(ctx.dtype)


def indexer_scores(qi, ki_all, wi):
    dots = jnp.einsum('bqhd,bsd->bqhs', qi.astype(jnp.float32), ki_all.astype(jnp.float32)) * (IDX_DIM ** -0.5)
    w = wi.astype(jnp.float32) * (IDX_HEADS ** -0.5)
    return jnp.einsum('bqh,bqhs->bqs', w, jax.nn.relu(dots))


def gather_rows(rows, idx):
    return jax.vmap(lambda r, i: r[i])(rows, idx)


def sparse_attend(q, kg, vg, sel, qpos):
    B, Q = q.shape[:2]
    logits = jnp.einsum('bqhd,bqkhd->bhqk', q.astype(jnp.float32), kg.astype(jnp.float32)) * (HEAD_DIM ** -0.5)
    ok = sel <= qpos[None, :, None]
    logits = jnp.where(ok[:, None], logits, -jnp.inf)
    p = jax.nn.softmax(logits, axis=-1)
    out = jnp.einsum('bhqk,bqkhd->bqhd', p, vg.astype(jnp.float32))
    return out.reshape(B, Q, ATTN_WIDTH).astype(q.dtype)


def dsa_prompt(q, k, v, qi, ki, wi, topk):
    B, S = q.shape[:2]
    key_pos = jnp.arange(S)

    def block(i):
        s0 = i * QBLOCK
        qpos = s0 + jnp.arange(QBLOCK)
        qb = lax.dynamic_slice_in_dim(q, s0, QBLOCK, axis=1)
        qib = lax.dynamic_slice_in_dim(qi, s0, QBLOCK, axis=1)
        wib = lax.dynamic_slice_in_dim(wi, s0, QBLOCK, axis=1)
        score = indexer_scores(qib, ki, wib)
        score = jnp.where(key_pos[None, None, :] <= qpos[None, :, None], score, -jnp.inf)
        _, sel = lax.top_k(score, topk)
        return sparse_attend(qb, gather_rows(k, sel), gather_rows(v, sel), sel, qpos)

    out = lax.map(block, jnp.arange(S // QBLOCK))
    return jnp.swapaxes(out, 0, 1).reshape(B, S, ATTN_WIDTH)


def dsa_sample(q, k, v, qi, ki, wi, ck, cv, cik, page_table, topk):
    DB, T = q.shape[:2]
    past = page_table.shape[1] * PAGE_SIZE
    ki_past = cik[page_table].reshape(DB, past, IDX_DIM).astype(ki.dtype)
    ki_all = jnp.concatenate([ki_past, ki], axis=1)
    qpos = past + jnp.arange(T)
    score = indexer_scores(qi, ki_all, wi)
    score = jnp.where(jnp.arange(past + T)[None, None, :] <= qpos[None, :, None], score, -jnp.inf)
    _, sel = lax.top_k(score, topk)
    in_past = (sel < past)[..., None, None]
    psel = jnp.minimum(sel, past - 1)
    phys = jnp.take_along_axis(page_table, (psel // PAGE_SIZE).reshape(DB, -1), axis=1).reshape(sel.shape)
    off = psel % PAGE_SIZE
    nsel = jnp.clip(sel - past, 0, T - 1)
    kg = jnp.where(in_past, ck[phys, off].astype(k.dtype), gather_rows(k, nsel))
    vg = jnp.where(in_past, cv[phys, off].astype(v.dtype), gather_rows(v, nsel))
    return sparse_attend(q, kg, vg, sel, qpos)


def merge_branches(a, b, ga, gb, w_a, w_b, w_o):
    return (jax.nn.sigmoid(ga) * (a @ w_a) + jax.nn.sigmoid(gb) * (b @ w_b)) @ w_o


def peer_block(xb, wq, k1, k2, eu, ev):
    n = xb.shape[0]
    q = (xb @ wq).reshape(n, PEER_HEADS, 2, PEER_HALF).astype(jnp.float32)
    s1 = jnp.einsum('nhd,kd->nhk', q[:, :, 0], k1.astype(jnp.float32))
    s2 = jnp.einsum('nhd,kd->nhk', q[:, :, 1], k2.astype(jnp.float32))
    v1, i1 = lax.top_k(s1, PEER_TOPK)
    v2, i2 = lax.top_k(s2, PEER_TOPK)
    cand = (v1[..., :, None] + v2[..., None, :]).reshape(n, PEER_HEADS, PEER_TOPK * PEER_TOPK)
    sc, ci = lax.top_k(cand, PEER_TOPK)
    e_idx = (jnp.take_along_axis(i1, ci // PEER_TOPK, axis=-1) * N_KEYS
             + jnp.take_along_axis(i2, ci % PEER_TOPK, axis=-1))
    g = jax.nn.softmax(sc, axis=-1)
    act = jax.nn.gelu(jnp.einsum('nd,nhkd->nhk', xb.astype(jnp.float32), eu[e_idx].astype(jnp.float32)), approximate=False)
    y = jnp.einsum('nhk,nhkd->nd', g * act, ev[e_idx].astype(jnp.float32))
    return y.astype(xb.dtype)


def peer_ffn(h, wq, k1, k2, eu, ev):
    B, T, D = h.shape
    n = B * T
    xr = h.reshape(n, D)
    f = lambda xb: peer_block(xb, wq, k1, k2, eu, ev)
    if n % PEER_BLOCK == 0:
        y = lax.map(f, xr.reshape(n // PEER_BLOCK, PEER_BLOCK, D)).reshape(n, D)
    else:
        y = f(xr)
    return y.reshape(B, T, D)


def setup_inputs(seed: int = 0) -> dict:
    key = jax.random.key(seed)
    ks = jax.random.split(key, 24)
    n_pages = PAST_LEN // PAGE_SIZE
    n_pool = (5 * DEC_BATCH * n_pages) // 4
    f32 = jnp.float32

    def nrm(k, shape, scale):
        return jax.random.normal(k, shape, f32) * scale

    page_table = jax.random.permutation(ks[0], n_pool)[: DEC_BATCH * n_pages].reshape(DEC_BATCH, n_pages).astype(jnp.int32)
    return {
        'x_prompt': nrm(ks[1], (BATCH, SEQ, D_MODEL), 1.0),
        'x_sample': nrm(ks[2], (DEC_BATCH, DEC_SEQ, D_MODEL), 1.0),
        'cache_k': nrm(ks[3], (DEPTH, n_pool, PAGE_SIZE, N_HEADS, HEAD_DIM), 1.0),
        'cache_v': nrm(ks[4], (DEPTH, n_pool, PAGE_SIZE, N_HEADS, HEAD_DIM), 1.0),
        'cache_idx_k': nrm(ks[5], (DEPTH, n_pool, PAGE_SIZE, IDX_DIM), 1.0),
        'state_pool': nrm(ks[6], (DEPTH, DEC_BATCH, POOL_BUF, POOL_WIDTH), 1.0),
        'page_table': page_table,
        'norm_mix': 1.0 + nrm(ks[7], (DEPTH, D_MODEL), 0.05),
        'w_in': nrm(ks[8], (DEPTH, D_MODEL, IN_WIDTH), D_MODEL ** -0.5),
        'pool_w': nrm(ks[9], (DEPTH, POOL_GROUPS, POOL_GROUP, POOL_GROUP), POOL_GROUP ** -0.5),
        'pool_scale': 1.0 + nrm(ks[10], (DEPTH, POOL_WIDTH), 0.1),
        'w_branch_a': nrm(ks[11], (DEPTH, POOL_WIDTH, D_MODEL), POOL_WIDTH ** -0.5),
        'w_branch_b': nrm(ks[12], (DEPTH, ATTN_WIDTH, D_MODEL), ATTN_WIDTH ** -0.5),
        'w_out': nrm(ks[13], (DEPTH, D_MODEL, D_MODEL), D_MODEL ** -0.5),
        'norm_ffn': 1.0 + nrm(ks[14], (DEPTH, D_MODEL), 0.05),
        'peer_wq': nrm(ks[15], (DEPTH, D_MODEL, PEER_HEADS * PEER_QDIM), D_MODEL ** -0.5),
        'peer_k1': nrm(ks[16], (DEPTH, N_KEYS, PEER_HALF), PEER_HALF ** -0.5),
        'peer_k2': nrm(ks[17], (DEPTH, N_KEYS, PEER_HALF), PEER_HALF ** -0.5),
        'peer_u': nrm(ks[18], (DEPTH, N_EXPERTS, D_MODEL), D_MODEL ** -0.5),
        'peer_v': nrm(ks[19], (DEPTH, N_EXPERTS, D_MODEL), 0.2),
        'norm_final': 1.0 + nrm(ks[20], (D_MODEL,), 0.05),
    }


def reference(x_prompt, x_sample, cache_k, cache_v, cache_idx_k, state_pool, page_table,
              norm_mix, w_in, pool_w, pool_scale, w_branch_a, w_branch_b, w_out,
              norm_ffn, peer_wq, peer_k1, peer_k2, peer_u, peer_v, norm_final):
    B, S, _ = x_prompt.shape
    T = x_sample.shape[1]
    past = page_table.shape[1] * PAGE_SIZE
    pos_p = jnp.arange(S, dtype=jnp.int32)
    pos_s = past + jnp.arange(T, dtype=jnp.int32)
    topk_p = min(TOPK_MAX, S // 4)
    topk_s = min(TOPK_MAX, (past + T) // 4)
    xp, xs = x_prompt, x_sample
    kp_l, vp_l, ikp_l, pp_l = [], [], [], []
    ks_l, vs_l, iks_l, ps_l = [], [], [], []
    for l in range(DEPTH):
        hp = rmsnorm(xp, norm_mix[l])
        u, q, k, v, qi, ki, wi, ga, gb = project_inputs(hp, w_in[l], pos_p)
        ctx = jnp.concatenate([jnp.zeros((B, POOL_BUF, POOL_WIDTH), u.dtype), u], axis=1)
        a = pool_mix(ctx, pos_p, pool_w[l], pool_scale[l])
        b = dsa_prompt(q, k, v, qi, ki, wi, topk_p)
        xp = xp + merge_branches(a, b, ga, gb, w_branch_a[l], w_branch_b[l], w_out[l])
        xp = xp + peer_ffn(rmsnorm(xp, norm_ffn[l]), peer_wq[l], peer_k1[l], peer_k2[l], peer_u[l], peer_v[l])
        kp_l.append(k)
        vp_l.append(v)
        ikp_l.append(ki)
        pp_l.append(ctx[:, -POOL_BUF:])
        hs = rmsnorm(xs, norm_mix[l])
        u, q, k, v, qi, ki, wi, ga, gb = project_inputs(hs, w_in[l], pos_s)
        ctx = jnp.concatenate([state_pool[l].astype(u.dtype), u], axis=1)
        a = pool_mix(ctx, pos_s, pool_w[l], pool_scale[l])
        b = dsa_sample(q, k, v, qi, ki, wi, cache_k[l], cache_v[l], cache_idx_k[l], page_table, topk_s)
        xs = xs + merge_branches(a, b, ga, gb, w_branch_a[l], w_branch_b[l], w_out[l])
        xs = xs + peer_ffn(rmsnorm(xs, norm_ffn[l]), peer_wq[l], peer_k1[l], peer_k2[l], peer_u[l], peer_v[l])
        ks_l.append(k)
        vs_l.append(v)
        iks_l.append(ki)
        ps_l.append(ctx[:, -POOL_BUF:])
    y_prompt = rmsnorm(xp, norm_final)
    y_sample = rmsnorm(xs, norm_final)
    return (y_prompt, y_sample,
            jnp.stack(kp_l), jnp.stack(vp_l), jnp.stack(ikp_l), jnp.stack(pp_l),
            jnp.stack(ks_l), jnp.stack(vs_l), jnp.stack(iks_l), jnp.stack(ps_l))
```

```python
import functools
import math

import jax
import jax.numpy as jnp
from jax import lax
from jax.experimental import pallas as pl
from jax.experimental.pallas import tpu as pltpu

D_MODEL = 2048
PAGE_SIZE = 128
POOL_WINDOWS = (2, 4, 8, 16)
POOL_GROUP = 256
POOL_WIDTH = len(POOL_WINDOWS) * POOL_GROUP
POOL_BUF = max(POOL_WINDOWS) - 1
N_HEADS = 8
HEAD_DIM = 128
ATTN_WIDTH = N_HEADS * HEAD_DIM
IDX_HEADS = 16
IDX_DIM = 64
TOPK_MAX = 256
ROPE_THETA = 10000.0
N_KEYS = 128
PEER_HEADS = 8
PEER_HALF = 128
PEER_TOPK = 16
EPS = 1e-6

V7X_LANES = 128
V7X_SUBLANES = 8
V7X_VMEM_BYTES = 64 * 1024 * 1024
TAIL_ROWS = 16
VMEM_LIMIT = 56 * 1024 * 1024

F32 = jnp.float32
BF16 = jnp.bfloat16
I32 = jnp.int32
NEG_BIG = -1e30
INT_MIN = -(2 ** 31)
NEG_INF_KEY = INT_MIN + 0x7FFFFF
NO_CUT = 2 ** 30


def _cparams(sem):
    return pltpu.CompilerParams(dimension_semantics=sem, vmem_limit_bytes=VMEM_LIMIT)


def _sort_key(x):
    b = lax.bitcast_convert_type(x, I32)
    return b ^ ((b >> 31) & 0x7FFFFFFF)


def _rmsnorm_kernel(x_ref, g_ref, o_ref):
    x = x_ref[...]
    var = jnp.mean(x * x, axis=-1, keepdims=True)
    o_ref[...] = (x * lax.rsqrt(var + EPS) * g_ref[...]).astype(o_ref.dtype)


def rmsnorm_rows(x, g, out_dtype, tm):
    m, d = x.shape
    return pl.pallas_call(
        _rmsnorm_kernel,
        grid=(m // tm,),
        in_specs=[pl.BlockSpec((tm, d), lambda i: (i, 0)),
                  pl.BlockSpec((1, d), lambda i: (0, 0))],
        out_specs=pl.BlockSpec((tm, d), lambda i: (i, 0)),
        out_shape=jax.ShapeDtypeStruct((m, d), out_dtype),
        compiler_params=_cparams(("parallel",)),
        name="rmsnorm",
    )(x, g.reshape(1, d))


def _add_rmsnorm_kernel(x_ref, y_ref, g_ref, o_ref):
    x = x_ref[...] + y_ref[...]
    var = jnp.mean(x * x, axis=-1, keepdims=True)
    o_ref[...] = x * lax.rsqrt(var + EPS) * g_ref[...]


def add_rmsnorm_rows(x, y, g, tm):
    m, d = x.shape
    return pl.pallas_call(
        _add_rmsnorm_kernel,
        grid=(m // tm,),
        in_specs=[pl.BlockSpec((tm, d), lambda i: (i, 0)),
                  pl.BlockSpec((tm, d), lambda i: (i, 0)),
                  pl.BlockSpec((1, d), lambda i: (0, 0))],
        out_specs=pl.BlockSpec((tm, d), lambda i: (i, 0)),
        out_shape=jax.ShapeDtypeStruct((m, d), F32),
        compiler_params=_cparams(("parallel",)),
        name="add_rmsnorm",
    )(x, y, g.reshape(1, d))


def _proj_kernel(*refs, rope, scale, out_f32, out_bf16):
    a_ref, w_ref = refs[:2]
    if rope is None:
        outs = refs[2:]
    else:
        c_ref, s_ref = refs[2:4]
        outs = refs[4:]
    acc = jnp.dot(a_ref[...], w_ref[...], preferred_element_type=F32)
    tn = acc.shape[1]
    for hh in range(tn // V7X_LANES):
        lo, hi = hh * V7X_LANES, (hh + 1) * V7X_LANES
        x = acc[:, lo:hi]
        if rope == "full":
            x = x * c_ref[...] + pltpu.roll(x, HEAD_DIM // 2, 1) * s_ref[...]
        elif rope == "half64":
            lane = lax.broadcasted_iota(I32, x.shape, 1)
            partner = jnp.where((lane & (IDX_DIM - 1)) < IDX_DIM // 2,
                                pltpu.roll(x, V7X_LANES - IDX_DIM // 2, 1),
                                pltpu.roll(x, IDX_DIM // 2, 1))
            x = x * c_ref[...] + partner * s_ref[...]
        if scale != 1.0:
            x = x * scale
        k = 0
        if out_f32:
            outs[k][:, lo:hi] = x
            k += 1
        if out_bf16:
            outs[k][:, lo:hi] = x.astype(BF16)


def proj(a, w, *, tm, tn, rope=None, tables=None, scale=1.0, out_f32=True, out_bf16=False):
    m, kdim = a.shape
    n = w.shape[1]
    in_specs = [pl.BlockSpec((tm, kdim), lambda i, j: (i, 0)),
                pl.BlockSpec((kdim, tn), lambda i, j: (0, j))]
    args = [a, w]
    if rope is not None:
        in_specs += [pl.BlockSpec((tm, V7X_LANES), lambda i, j: (i, 0))] * 2
        args += list(tables)
    out_shape, out_specs = [], []
    for flag, dt in ((out_f32, F32), (out_bf16, BF16)):
        if flag:
            out_shape.append(jax.ShapeDtypeStruct((m, n), dt))
            out_specs.append(pl.BlockSpec((tm, tn), lambda i, j: (i, j)))
    res = pl.pallas_call(
        functools.partial(_proj_kernel, rope=rope, scale=scale, out_f32=out_f32, out_bf16=out_bf16),
        grid=(m // tm, n // tn),
        in_specs=in_specs,
        out_specs=out_specs,
        out_shape=out_shape,
        compiler_params=_cparams(("parallel", "parallel")),
        name="proj_" + (rope or "plain"),
    )(*args)
    return res


def _pool_prompt_kernel(u_ref, pw_ref, ps_ref, o_ref, ext_ref):
    i = pl.program_id(0)
    tm = u_ref.shape[0]
    halo = 2 * V7X_SUBLANES

    @pl.when(i == 0)
    def _():
        ext_ref[0:halo, :] = jnp.zeros((halo, POOL_WIDTH), F32)

    ext_ref[halo:halo + tm, :] = u_ref[...]
    pos = i * tm + lax.broadcasted_iota(I32, (tm, 1), 0)
    for g, w in enumerate(POOL_WINDOWS):
        lo, hi = g * POOL_GROUP, (g + 1) * POOL_GROUP
        tok = ext_ref[halo:halo + tm, lo:hi]
        wsum = tok
        for j in range(1, w):
            wsum = wsum + ext_ref[halo - j:halo - j + tm, lo:hi]
        cnt = jnp.minimum(w, pos + 1).astype(F32)
        pooled = wsum / cnt - tok
        mixed = jnp.dot(pooled.astype(BF16), pw_ref[g], preferred_element_type=F32)
        o_ref[:, lo:hi] = (mixed * ps_ref[:, lo:hi]).astype(BF16)
    ext_ref[0:halo, :] = ext_ref[tm:tm + halo, :]


def pool_prompt(u_src, pool_w, pool_scale, s, tm):
    return pl.pallas_call(
        _pool_prompt_kernel,
        grid=(s // tm,),
        in_specs=[pl.BlockSpec((tm, POOL_WIDTH), lambda i: (i, 0)),
                  pl.BlockSpec((len(POOL_WINDOWS), POOL_GROUP, POOL_GROUP), lambda i: (0, 0, 0)),
                  pl.BlockSpec((1, POOL_WIDTH), lambda i: (0, 0))],
        out_specs=pl.BlockSpec((tm, POOL_WIDTH), lambda i: (i, 0)),
        out_shape=jax.ShapeDtypeStruct((s, POOL_WIDTH), BF16),
        scratch_shapes=[pltpu.VMEM((tm + 2 * V7X_SUBLANES, POOL_WIDTH), F32)],
        compiler_params=_cparams(("arbitrary",)),
        name="pool_prompt",
    )(u_src, pool_w, pool_scale.reshape(1, POOL_WIDTH))


def _pool_sample_kernel(st_ref, u_ref, pw_ref, ps_ref, o_ref, *, cnts):
    for g, w in enumerate(POOL_WINDOWS):
        lo, hi = g * POOL_GROUP, (g + 1) * POOL_GROUP
        tok = u_ref[:, lo:hi]
        wsum = tok
        for j in range(1, w):
            wsum = wsum + st_ref[POOL_BUF - j, :, lo:hi]
        pooled = wsum / cnts[g] - tok
        mixed = jnp.dot(pooled.astype(BF16), pw_ref[g], preferred_element_type=F32)
        o_ref[:, lo:hi] = (mixed * ps_ref[:, lo:hi]).astype(BF16)


def pool_sample(state_t, u_src, pool_w, pool_scale, past):
    nb = state_t.shape[1]
    cnts = tuple(float(min(w, past + 1)) for w in POOL_WINDOWS)
    return pl.pallas_call(
        functools.partial(_pool_sample_kernel, cnts=cnts),
        grid=(1,),
        in_specs=[pl.BlockSpec((POOL_BUF, nb, POOL_WIDTH), lambda i: (0, 0, 0)),
                  pl.BlockSpec((nb, POOL_WIDTH), lambda i: (0, 0)),
                  pl.BlockSpec((len(POOL_WINDOWS), POOL_GROUP, POOL_GROUP), lambda i: (0, 0, 0)),
                  pl.BlockSpec((1, POOL_WIDTH), lambda i: (0, 0))],
        out_specs=pl.BlockSpec((nb, POOL_WIDTH), lambda i: (0, 0)),
        out_shape=jax.ShapeDtypeStruct((nb, POOL_WIDTH), BF16),
        compiler_params=_cparams(("arbitrary",)),
        name="pool_sample",
    )(state_t, u_src, pool_w, pool_scale.reshape(1, POOL_WIDTH))


def _dsa_prompt_kernel(q_ref, qi_ref, w_ref, kit_ref, k_hbm, v_hbm, o_ref,
                       key_s, k_s, v_s, m_s, l_s, acc_s, cut_s, sem, *, topk, tk):
    i = pl.program_id(0)
    tq = q_ref.shape[0]

    @pl.when(i == 0)
    def _():
        ck = pltpu.make_async_copy(k_hbm, k_s, sem.at[0])
        cv = pltpu.make_async_copy(v_hbm, v_s, sem.at[1])
        ck.start()
        cv.start()
        ck.wait()
        cv.wait()

    nck = ((i + 1) * tq + tk - 1) // tk
    qpos = i * tq + lax.broadcasted_iota(I32, (tq, 1), 0)
    lane = lax.broadcasted_iota(I32, (tq, tk), 1)

    def score_body(c, carry):
        kit = kit_ref[c]
        acc = jnp.zeros((tq, tk), F32)
        for h in range(IDX_HEADS):
            d = jnp.dot(qi_ref[h], kit, preferred_element_type=F32)
            acc = acc + jnp.maximum(d, 0.0) * w_ref[:, IDX_DIM + h:IDX_DIM + h + 1]
        acc = jnp.where(c * tk + lane <= qpos, acc, -jnp.inf)
        key_s[c] = _sort_key(acc)
        return carry

    lax.fori_loop(0, nck, score_body, 0)

    def count_rows(pred):
        def body(c, cnt):
            kc = key_s[c]
            hit = jnp.where(pred(kc, c * tk + lane), 1.0, 0.0)
            for l in range(tk // V7X_LANES):
                cnt = cnt + hit[:, l * V7X_LANES:(l + 1) * V7X_LANES]
            return cnt
        cnt = lax.fori_loop(0, nck, body, jnp.zeros((tq, V7X_LANES), F32))
        return jnp.sum(cnt, axis=1, keepdims=True)

    def thr_body(it, thr):
        cand = thr + jnp.left_shift(jnp.int32(1), 31 - it)
        tot = count_rows(lambda kc, kp: kc >= cand)
        return jnp.where(tot >= topk, cand, thr)

    thr = lax.fori_loop(0, 32, thr_body, jnp.full((tq, 1), INT_MIN, I32))

    n_gt = count_rows(lambda kc, kp: kc > thr)
    n_eq = count_rows(lambda kc, kp: kc == thr)
    need = topk - n_gt
    tied = jnp.logical_and(n_eq > need, thr != NEG_INF_KEY)
    cut_s[...] = jnp.full((tq, 1), NO_CUT, I32)

    @pl.when(jnp.max(jnp.where(tied, 1.0, 0.0)) > 0.0)
    def _():
        def cut_body(it, cut):
            cand = cut + jnp.left_shift(jnp.int32(1), 13 - it)
            tot = count_rows(lambda kc, kp: jnp.logical_and(kc == thr, kp < cand))
            return jnp.where(tot <= need, cand, cut)
        cut_s[...] = lax.fori_loop(0, 14, cut_body, jnp.zeros((tq, 1), I32))

    cut = cut_s[...]

    m_s[...] = jnp.full(m_s.shape, NEG_BIG, F32)
    l_s[...] = jnp.zeros(l_s.shape, F32)
    acc_s[...] = jnp.zeros(acc_s.shape, F32)

    def att_body(c, carry):
        kc = key_s[c]
        kpos = c * tk + lane
        sel = jnp.logical_or(kc > thr, jnp.logical_and(kc == thr, kpos < cut))
        msk = jnp.logical_and(sel, kpos <= qpos)
        row0 = pl.multiple_of(c * tk, tk)
        for h in range(N_HEADS):
            lo, hi = h * HEAD_DIM, (h + 1) * HEAD_DIM
            s = lax.dot_general(q_ref[:, lo:hi], k_s[pl.ds(row0, tk), lo:hi],
                                (((1,), (1,)), ((), ())), preferred_element_type=F32)
            s = jnp.where(msk, s, NEG_BIG)
            m_old = m_s[h]
            m_new = jnp.maximum(m_old, jnp.max(s, axis=1, keepdims=True))
            p = jnp.where(msk, jnp.exp(s - m_new), 0.0)
            alpha = jnp.exp(m_old - m_new)
            l_s[h] = alpha * l_s[h] + jnp.sum(p, axis=1, keepdims=True)
            acc_s[h] = alpha * acc_s[h] + jnp.dot(p.astype(BF16), v_s[pl.ds(row0, tk), lo:hi],
                                                  preferred_element_type=F32)
            m_s[h] = m_new
        return carry

    lax.fori_loop(0, nck, att_body, 0)
    for h in range(N_HEADS):
        o_ref[:, h * HEAD_DIM:(h + 1) * HEAD_DIM] = (acc_s[h] / l_s[h]).astype(BF16)


def dsa_prompt(q, qi_h, w_small, kit, k, v, *, topk, tq, tk):
    s = q.shape[0]
    return pl.pallas_call(
        functools.partial(_dsa_prompt_kernel, topk=topk, tk=tk),
        grid=(s // tq,),
        in_specs=[pl.BlockSpec((tq, ATTN_WIDTH), lambda i: (i, 0)),
                  pl.BlockSpec((IDX_HEADS, tq, IDX_DIM), lambda i: (0, i, 0)),
                  pl.BlockSpec((tq, V7X_LANES), lambda i: (i, 0)),
                  pl.BlockSpec((s // tk, IDX_DIM, tk), lambda i: (0, 0, 0)),
                  pl.BlockSpec(memory_space=pl.ANY),
                  pl.BlockSpec(memory_space=pl.ANY)],
        out_specs=pl.BlockSpec((tq, ATTN_WIDTH), lambda i: (i, 0)),
        out_shape=jax.ShapeDtypeStruct((s, ATTN_WIDTH), BF16),
        scratch_shapes=[pltpu.VMEM((s // tk, tq, tk), I32),
                        pltpu.VMEM((s, ATTN_WIDTH), BF16),
                        pltpu.VMEM((s, ATTN_WIDTH), BF16),
                        pltpu.VMEM((N_HEADS, tq, 1), F32),
                        pltpu.VMEM((N_HEADS, tq, 1), F32),
                        pltpu.VMEM((N_HEADS, tq, HEAD_DIM), F32),
                        pltpu.VMEM((tq, 1), I32),
                        pltpu.SemaphoreType.DMA((2,))],
        compiler_params=_cparams(("arbitrary",)),
        name="dsa_prompt",
    )(q, qi_h, w_small, kit, k, v)


def _sample_score_kernel(pt_ref, qit_ref, w_ref, kn_ref, *rest, n_pages):
    pages = rest[:n_pages]
    o_ref = rest[n_pages]
    qit = qit_ref[0]
    wrow = w_ref[0]

    def score(keys):
        d = jnp.dot(keys.astype(BF16), qit, preferred_element_type=F32)
        return jnp.sum(jnp.maximum(d, 0.0) * wrow, axis=1, keepdims=True)

    for j in range(n_pages):
        o_ref[0, j * PAGE_SIZE:(j + 1) * PAGE_SIZE, :] = score(pages[j][0])
    past = n_pages * PAGE_SIZE
    sn = score(kn_ref[0][:, :IDX_DIM])
    row = lax.broadcasted_iota(I32, sn.shape, 0)
    o_ref[0, past:past + TAIL_ROWS, :] = jnp.where(row == 0, sn, -jnp.inf)


def sample_scores(page_table, qit, w3, kn, cache_idx_k):
    nb, n_pages = page_table.shape
    rows = n_pages * PAGE_SIZE + TAIL_ROWS
    page_specs = [pl.BlockSpec((1, PAGE_SIZE, IDX_DIM), functools.partial(lambda b, pt, j: (pt[b, j], 0, 0), j=j))
                  for j in range(n_pages)]
    grid_spec = pltpu.PrefetchScalarGridSpec(
        num_scalar_prefetch=1,
        grid=(nb,),
        in_specs=[pl.BlockSpec((1, IDX_DIM, IDX_HEADS), lambda b, pt: (b, 0, 0)),
                  pl.BlockSpec((1, 1, IDX_HEADS), lambda b, pt: (b, 0, 0)),
                  pl.BlockSpec((1, TAIL_ROWS, V7X_LANES), lambda b, pt: (b, 0, 0))] + page_specs,
        out_specs=pl.BlockSpec((1, rows, 1), lambda b, pt: (b, 0, 0)),
    )
    return pl.pallas_call(
        functools.partial(_sample_score_kernel, n_pages=n_pages),
        grid_spec=grid_spec,
        out_shape=jax.ShapeDtypeStruct((nb, rows, 1), F32),
        compiler_params=_cparams(("arbitrary",)),
        name="sample_scores",
    )(page_table, qit, w3, kn, *([cache_idx_k] * n_pages))


def _sample_thr_kernel(sc_ref, thr_ref, cut_ref, *, topk):
    key = _sort_key(sc_ref[...])
    rows = key.shape[0]
    kpos = lax.broadcasted_iota(I32, key.shape, 0)

    def count(hit):
        return jnp.sum(jnp.where(hit, 1.0, 0.0), axis=0, keepdims=True)

    def thr_body(it, thr):
        cand = thr + jnp.left_shift(jnp.int32(1), 31 - it)
        return jnp.where(count(key >= cand) >= topk, cand, thr)

    thr = lax.fori_loop(0, 32, thr_body, jnp.full((1, key.shape[1]), INT_MIN, I32))
    need = topk - count(key > thr)
    nbits = max(1, int(rows).bit_length())

    def cut_body(it, cut):
        cand = cut + jnp.left_shift(jnp.int32(1), nbits - 1 - it)
        tot = count(jnp.logical_and(key == thr, kpos < cand))
        return jnp.where(tot <= need, cand, cut)

    cut = lax.fori_loop(0, nbits, cut_body, jnp.zeros((1, key.shape[1]), I32))
    thr_ref[...] = thr
    cut_ref[...] = cut


def sample_thresholds(score_t, topk):
    rows, nb = score_t.shape
    return pl.pallas_call(
        functools.partial(_sample_thr_kernel, topk=topk),
        grid=(1,),
        in_specs=[pl.BlockSpec((rows, nb), lambda i: (0, 0))],
        out_specs=[pl.BlockSpec((1, nb), lambda i: (0, 0))] * 2,
        out_shape=[jax.ShapeDtypeStruct((1, nb), I32)] * 2,
        compiler_params=_cparams(("arbitrary",)),
        name="sample_thr",
    )(score_t)


def _sample_attn_kernel(pt_ref, thr_ref, cut_ref, sc_ref, qcol_ref, kn_ref, vn_ref, *rest, n_pages):
    kpages = rest[:n_pages]
    vpages = rest[n_pages:2 * n_pages]
    o_ref = rest[2 * n_pages]
    lg_s = rest[2 * n_pages + 1]
    b = pl.program_id(0)
    past = n_pages * PAGE_SIZE
    rows = past + TAIL_ROWS

    r_i = lax.broadcasted_iota(I32, (ATTN_WIDTH, V7X_LANES), 0)
    c_i = lax.broadcasted_iota(I32, (ATTN_WIDTH, V7X_LANES), 1)
    qbd = jnp.where(r_i // HEAD_DIM == c_i, qcol_ref[0], 0.0).astype(BF16)
    e_r = lax.broadcasted_iota(I32, (V7X_LANES, ATTN_WIDTH), 0)
    e_c = lax.broadcasted_iota(I32, (V7X_LANES, ATTN_WIDTH), 1)
    expand = jnp.where(e_c // HEAD_DIM == e_r, 1.0, 0.0).astype(BF16)

    for j in range(n_pages):
        lg_s[j * PAGE_SIZE:(j + 1) * PAGE_SIZE, :] = jnp.dot(
            kpages[j][0].astype(BF16), qbd, preferred_element_type=F32)
    lg_s[past:rows, :] = jnp.dot(kn_ref[0].astype(BF16), qbd, preferred_element_type=F32)

    key = _sort_key(sc_ref[0])
    kpos = lax.broadcasted_iota(I32, key.shape, 0)
    thr = thr_ref[b]
    cut = cut_ref[b]
    sel = jnp.logical_or(key > thr, jnp.logical_and(key == thr, kpos < cut))
    msk = jnp.logical_and(sel, kpos <= past)
    s = jnp.where(msk, lg_s[...], NEG_BIG)
    mx = jnp.max(s, axis=0, keepdims=True)
    p = jnp.where(msk, jnp.exp(s - mx), 0.0)
    p = (p / jnp.sum(p, axis=0, keepdims=True)).astype(BF16)

    out = jnp.zeros((1, ATTN_WIDTH), F32)
    for j in range(n_pages):
        pe = jnp.dot(p[j * PAGE_SIZE:(j + 1) * PAGE_SIZE, :], expand, preferred_element_type=F32)
        out = out + jnp.sum(pe * vpages[j][0], axis=0, keepdims=True)
    pe = jnp.dot(p[past:rows, :], expand, preferred_element_type=F32)
    out = out + jnp.sum(pe * vn_ref[0], axis=0, keepdims=True)
    o_ref[0] = out.astype(BF16)


def sample_attention(page_table, thr, cut, scores, qcol, kn, vn, cache_k, cache_v):
    nb, n_pages = page_table.shape
    rows = n_pages * PAGE_SIZE + TAIL_ROWS
    pidx = lambda j: functools.partial(lambda b, pt, th, cu, j: (pt[b, j], 0, 0), j=j)
    page_specs = [pl.BlockSpec((1, PAGE_SIZE, ATTN_WIDTH), pidx(j)) for j in range(n_pages)]
    per_b = lambda b, pt, th, cu: (b, 0, 0)
    grid_spec = pltpu.PrefetchScalarGridSpec(
        num_scalar_prefetch=3,
        grid=(nb,),
        in_specs=[pl.BlockSpec((1, rows, 1), per_b),
                  pl.BlockSpec((1, ATTN_WIDTH, 1), per_b),
                  pl.BlockSpec((1, TAIL_ROWS, ATTN_WIDTH), per_b),
                  pl.BlockSpec((1, TAIL_ROWS, ATTN_WIDTH), per_b)] + page_specs + page_specs,
        out_specs=pl.BlockSpec((1, 1, ATTN_WIDTH), per_b),
        scratch_shapes=[pltpu.VMEM((rows, V7X_LANES), F32)],
    )
    out = pl.pallas_call(
        functools.partial(_sample_attn_kernel, n_pages=n_pages),
        grid_spec=grid_spec,
        out_shape=jax.ShapeDtypeStruct((nb, 1, ATTN_WIDTH), BF16),
        compiler_params=_cparams(("arbitrary",)),
        name="sample_attn",
    )(page_table, thr, cut, scores, qcol, kn, vn, *([cache_k] * n_pages), *([cache_v] * n_pages))
    return out.reshape(nb, ATTN_WIDTH)


def _merge_kernel(a_ref, b_ref, wa_ref, wb_ref, ga_ref, gb_ref, o_ref):
    ya = jnp.dot(a_ref[...], wa_ref[...], preferred_element_type=F32)
    yb = jnp.dot(b_ref[...], wb_ref[...], preferred_element_type=F32)
    o_ref[...] = (jax.nn.sigmoid(ga_ref[...]) * ya + jax.nn.sigmoid(gb_ref[...]) * yb).astype(BF16)


def merge(a, b, wa, wb, gates, ga_blk, gb_blk, tm, tn):
    m = a.shape[0]
    return pl.pallas_call(
        _merge_kernel,
        grid=(m // tm, D_MODEL // tn),
        in_specs=[pl.BlockSpec((tm, POOL_WIDTH), lambda i, j: (i, 0)),
                  pl.BlockSpec((tm, ATTN_WIDTH), lambda i, j: (i, 0)),
                  pl.BlockSpec((POOL_WIDTH, tn), lambda i, j: (0, j)),
                  pl.BlockSpec((ATTN_WIDTH, tn), lambda i, j: (0, j)),
                  pl.BlockSpec((tm, tn), lambda i, j: (i, ga_blk + j)),
                  pl.BlockSpec((tm, tn), lambda i, j: (i, gb_blk + j))],
        out_specs=pl.BlockSpec((tm, tn), lambda i, j: (i, j)),
        out_shape=jax.ShapeDtypeStruct((m, D_MODEL), BF16),
        compiler_params=_cparams(("parallel", "parallel")),
        name="merge",
    )(a, b, wa, wb, gates, gates)


def _outproj_kernel(m_ref, w_ref, x_ref, g_ref, x2_ref, h2_ref):
    x2 = x_ref[...] + jnp.dot(m_ref[...], w_ref[...], preferred_element_type=F32)
    x2_ref[...] = x2
    var = jnp.mean(x2 * x2, axis=-1, keepdims=True)
    h2_ref[...] = (x2 * lax.rsqrt(var + EPS) * g_ref[...]).astype(BF16)


def outproj(mrg, w_o, x, g, tm):
    m = x.shape[0]
    return pl.pallas_call(
        _outproj_kernel,
        grid=(m // tm,),
        in_specs=[pl.BlockSpec((tm, D_MODEL), lambda i: (i, 0)),
                  pl.BlockSpec((D_MODEL, D_MODEL), lambda i: (0, 0)),
                  pl.BlockSpec((tm, D_MODEL), lambda i: (i, 0)),
                  pl.BlockSpec((1, D_MODEL), lambda i: (0, 0))],
        out_specs=[pl.BlockSpec((tm, D_MODEL), lambda i: (i, 0))] * 2,
        out_shape=[jax.ShapeDtypeStruct((m, D_MODEL), F32), jax.ShapeDtypeStruct((m, D_MODEL), BF16)],
        compiler_params=_cparams(("parallel",)),
        name="outproj",
    )(mrg, w_o, x, g.reshape(1, D_MODEL))


def _top_rows(x, n, out_ref=None):
    rows = []
    cur = x
    for r in range(n):
        mx = jnp.max(cur, axis=0, keepdims=True)
        rows.append(mx)
        if out_ref is not None:
            out_ref[r:r + 1, :] = mx
        if r + 1 < n:
            cur = jnp.where(cur == mx, -jnp.inf, cur)
    return rows


def _peer_kernel(qp_ref, h2t_ref, k1_ref, k2_ref, eu_ref, ev_ref, o_ref,
                 s1_s, f1_s, s2_s, e2_s, thr_s, v1_s, v2_s, wt_s):
    c = pl.program_id(1)
    ec = eu_ref.shape[0]
    nt = (((1,), (1,)), ((), ()))

    @pl.when(c == 0)
    def _():
        o_ref[...] = jnp.zeros(o_ref.shape, F32)
        for h in range(PEER_HEADS):
            b0 = h * 2 * PEER_HALF
            s1 = lax.dot_general(k1_ref[...], qp_ref[:, b0:b0 + PEER_HALF], nt, preferred_element_type=F32)
            s2 = lax.dot_general(k2_ref[...], qp_ref[:, b0 + PEER_HALF:b0 + 2 * PEER_HALF], nt,
                                 preferred_element_type=F32)
            r1 = _top_rows(s1, PEER_TOPK, v1_s)
            r2 = _top_rows(s2, PEER_TOPK, v2_s)
            v2 = v2_s[...]
            cand = jnp.concatenate([v1_s[a:a + 1, :] + v2 for a in range(PEER_TOPK)], axis=0)
            top = _top_rows(cand, PEER_TOPK)
            den = jnp.zeros_like(top[0])
            for t in top:
                den = den + jnp.exp(t - top[0])
            s1_s[h] = s1
            s2_s[h] = s2
            f1_s[h] = jnp.exp(s1 - r1[0]) / den
            e2_s[h] = jnp.exp(s2 - r2[0])
            thr_s[h] = jnp.broadcast_to(top[-1], thr_s.shape[1:])

    act = jnp.dot(eu_ref[...], h2t_ref[...], preferred_element_type=F32)
    gel = 0.5 * act * (1.0 + lax.erf(act * (1.0 / math.sqrt(2.0))))
    for t in range(ec // N_KEYS):
        i1 = c * (ec // N_KEYS) + t
        g = jnp.zeros((N_KEYS, act.shape[1]), F32)
        for h in range(PEER_HEADS):
            sgrid = s1_s[h, pl.ds(i1, 1), :] + s2_s[h]
            val = f1_s[h, pl.ds(i1, 1), :] * e2_s[h]
            g = g + jnp.where(sgrid >= thr_s[h, 0:1, :], val, 0.0)
        wt_s[t * N_KEYS:(t + 1) * N_KEYS, :] = (g * gel[t * N_KEYS:(t + 1) * N_KEYS, :]).astype(BF16)
    o_ref[...] += lax.dot_general(wt_s[...], ev_ref[...], (((0,), (0,)), ((), ())),
                                  preferred_element_type=F32)


def peer(qp, h2t, k1, k2, eu, ev, *, tb, ec):
    m = qp.shape[0]
    n_exp = eu.shape[0]
    return pl.pallas_call(
        _peer_kernel,
        grid=(m // tb, n_exp // ec),
        in_specs=[pl.BlockSpec((tb, D_MODEL), lambda i, c: (i, 0)),
                  pl.BlockSpec((D_MODEL, tb), lambda i, c: (0, i)),
                  pl.BlockSpec((N_KEYS, PEER_HALF), lambda i, c: (0, 0)),
                  pl.BlockSpec((N_KEYS, PEER_HALF), lambda i, c: (0, 0)),
                  pl.BlockSpec((ec, D_MODEL), lambda i, c: (c, 0)),
                  pl.BlockSpec((ec, D_MODEL), lambda i, c: (c, 0))],
        out_specs=pl.BlockSpec((tb, D_MODEL), lambda i, c: (i, 0)),
        out_shape=jax.ShapeDtypeStruct((m, D_MODEL), F32),
        scratch_shapes=[pltpu.VMEM((PEER_HEADS, N_KEYS, tb), F32),
                        pltpu.VMEM((PEER_HEADS, N_KEYS, tb), F32),
                        pltpu.VMEM((PEER_HEADS, N_KEYS, tb), F32),
                        pltpu.VMEM((PEER_HEADS, N_KEYS, tb), F32),
                        pltpu.VMEM((PEER_HEADS, V7X_SUBLANES, tb), F32),
                        pltpu.VMEM((PEER_TOPK, tb), F32),
                        pltpu.VMEM((PEER_TOPK, tb), F32),
                        pltpu.VMEM((ec, tb), BF16)],
        compiler_params=_cparams(("parallel", "arbitrary")),
        name="peer",
    )(qp, h2t, k1, k2, eu, ev)


def _rope_tables(pos):
    pos = pos.astype(F32)[:, None]
    half = HEAD_DIM // 2
    fr = ROPE_THETA ** (-jnp.arange(half, dtype=F32) / half)
    ang = pos * fr[None, :]
    c, s = jnp.cos(ang), jnp.sin(ang)
    c_full = jnp.concatenate([c, c], axis=1)
    s_full = jnp.concatenate([-s, s], axis=1)
    half_i = IDX_DIM // 2
    fr_i = ROPE_THETA ** (-jnp.arange(half_i, dtype=F32) / half_i)
    ang_i = pos * fr_i[None, :]
    ci, si = jnp.cos(ang_i), jnp.sin(ang_i)
    n = pos.shape[0]
    w_scale = jnp.full((n, IDX_HEADS), IDX_HEADS ** -0.5, F32)
    tail = V7X_LANES - IDX_DIM - IDX_HEADS
    c_small = jnp.concatenate([ci, ci, w_scale, jnp.ones((n, tail), F32)], axis=1)
    s_small = jnp.concatenate([-si, si, jnp.zeros((n, V7X_LANES - IDX_DIM), F32)], axis=1)
    c_qi = jnp.concatenate([ci, ci, ci, ci], axis=1)
    s_qi = jnp.concatenate([-si, si, -si, si], axis=1)
    return (c_full, s_full), (c_qi, s_qi), (c_small, s_small)


def _group_forward(x, pos, wts, tm):
    h = rmsnorm_rows(x, wts["norm_mix"], BF16, tm)
    full_t, qi_t, small_t = _rope_tables(pos)
    plain = proj(h, wts["w_plain"], tm=tm, tn=1024)[0]
    v32, v16 = proj(h, wts["w_v"], tm=tm, tn=1024, out_bf16=True)
    q16 = proj(h, wts["w_q"], tm=tm, tn=1024, rope="full", tables=full_t,
               scale=HEAD_DIM ** -0.5, out_f32=False, out_bf16=True)[0]
    k32, k16 = proj(h, wts["w_k"], tm=tm, tn=1024, rope="full", tables=full_t, out_bf16=True)
    qi16 = proj(h, wts["w_qi"], tm=tm, tn=1024, rope="half64", tables=qi_t,
                scale=IDX_DIM ** -0.5, out_f32=False, out_bf16=True)[0]
    small = proj(h, wts["w_small"], tm=tm, tn=V7X_LANES, rope="half64", tables=small_t)[0]
    return plain, v32, v16, q16, k32, k16, qi16, small


def _tail(x, a, b, plain, wts, tm, tb):
    mrg = merge(a, b, wts["w_a"], wts["w_b"], plain, 1, 3, tm, 1024)
    x2, h2 = outproj(mrg, wts["w_o"], x, wts["norm_ffn"], min(tm, 256))
    qp = proj(h2, wts["w_pq"], tm=tm, tn=1024, out_f32=False, out_bf16=True)[0]
    ffn = peer(qp, h2.T, wts["k1"], wts["k2"], wts["eu"], wts["ev"], tb=tb, ec=512)
    return add_rmsnorm_rows(x2, ffn, wts["norm_final"], min(tm, 256))


def kernel(x_prompt, x_sample, cache_k, cache_v, cache_idx_k, state_pool, page_table,
           norm_mix, w_in, pool_w, pool_scale, w_branch_a, w_branch_b, w_out,
           norm_ffn, peer_wq, peer_k1, peer_k2, peer_u, peer_v, norm_final):
    bsz, s, _ = x_prompt.shape
    nb, t_dec, _ = x_sample.shape
    depth = w_in.shape[0]
    assert bsz == 1 and t_dec == 1 and depth == 1, "kernel supports BATCH=1, DEC_SEQ=1, DEPTH=1"
    n_pages = page_table.shape[1]
    past = n_pages * PAGE_SIZE
    topk_p = min(TOPK_MAX, s // 4)
    topk_s = min(TOPK_MAX, (past + t_dec) // 4)

    wi = w_in[0]
    o_q, o_k, o_v, o_qi = POOL_WIDTH, POOL_WIDTH + ATTN_WIDTH, POOL_WIDTH + 2 * ATTN_WIDTH, POOL_WIDTH + 3 * ATTN_WIDTH
    o_ki = o_qi + IDX_HEADS * IDX_DIM
    o_g = o_ki + IDX_DIM + IDX_HEADS
    w_small = jnp.pad(wi[:, o_ki:o_g], ((0, 0), (0, V7X_LANES - IDX_DIM - IDX_HEADS)))
    wts = {
        "norm_mix": norm_mix[0], "norm_ffn": norm_ffn[0], "norm_final": norm_final,
        "w_plain": jnp.concatenate([wi[:, :o_q], wi[:, o_g:]], axis=1).astype(BF16),
        "w_q": wi[:, o_q:o_k].astype(BF16), "w_k": wi[:, o_k:o_v].astype(BF16),
        "w_v": wi[:, o_v:o_qi].astype(BF16), "w_qi": wi[:, o_qi:o_ki].astype(BF16),
        "w_small": w_small.astype(BF16),
        "w_a": w_branch_a[0].astype(BF16), "w_b": w_branch_b[0].astype(BF16), "w_o": w_out[0].astype(BF16),
        "w_pq": peer_wq[0].astype(BF16), "k1": peer_k1[0].astype(BF16), "k2": peer_k2[0].astype(BF16),
        "eu": peer_u[0].astype(BF16), "ev": peer_v[0].astype(BF16),
    }
    pool_w16 = pool_w[0].astype(BF16)

    xp = x_prompt[0]
    tm_p = 512
    plain, v32, v16, q16, k32, k16, qi16, small = _group_forward(xp, jnp.arange(s, dtype=I32), wts, tm_p)
    a_p = pool_prompt(plain, pool_w16, pool_scale[0], s, tm_p)
    tk = 512
    qi_h = qi16.reshape(s, IDX_HEADS, IDX_DIM).transpose(1, 0, 2)
    kit = small[:, :IDX_DIM].astype(BF16).reshape(s // tk, tk, IDX_DIM).transpose(0, 2, 1)
    b_p = dsa_prompt(q16, qi_h, small, kit, k16, v16, topk=topk_p, tq=256, tk=tk)
    y_p = _tail(xp, a_p, b_p, plain, wts, tm_p, 512)

    xs = x_sample[:, 0]
    pos_s = jnp.full((nb,), past, I32)
    plain_s, v32_s, _, q16_s, k32_s, _, qi16_s, small_s = _group_forward(xs, pos_s, wts, nb)
    a_s = pool_sample(state_pool[0].transpose(1, 0, 2), plain_s, pool_w16, pool_scale[0], past)
    qit = qi16_s.reshape(nb, IDX_HEADS, IDX_DIM).transpose(0, 2, 1)
    w3 = small_s[:, IDX_DIM:IDX_DIM + IDX_HEADS].reshape(nb, 1, IDX_HEADS)
    pad8 = lambda z: jnp.pad(z[:, None, :], ((0, 0), (0, TAIL_ROWS - 1), (0, 0)))
    scores = sample_scores(page_table, qit, w3, pad8(small_s), cache_idx_k[0])
    thr, cut = sample_thresholds(scores[:, :, 0].T, topk_s)
    ck = cache_k[0].reshape(-1, PAGE_SIZE, ATTN_WIDTH)
    cv = cache_v[0].reshape(-1, PAGE_SIZE, ATTN_WIDTH)
    qcol = q16_s.astype(F32).reshape(nb, ATTN_WIDTH, 1)
    b_s = sample_attention(page_table, thr[0], cut[0], scores, qcol, pad8(k32_s), pad8(v32_s), ck, cv)
    y_s = _tail(xs, a_s, b_s, plain_s, wts, nb, nb)

    u_p = plain[:, :POOL_WIDTH]
    pool_prompt_out = jnp.concatenate(
        [jnp.zeros((max(POOL_BUF - s, 0), POOL_WIDTH), F32), u_p[max(s - POOL_BUF, 0):]], axis=0)
    pool_sample_out = jnp.concatenate([state_pool[0][:, 1:], plain_s[:, None, :POOL_WIDTH]], axis=1)
    return (y_p[None], y_s[:, None, :],
            k32.reshape(1, 1, s, N_HEADS, HEAD_DIM), v32.reshape(1, 1, s, N_HEADS, HEAD_DIM),
            small[:, :IDX_DIM].reshape(1, 1, s, IDX_DIM), pool_prompt_out[None, None],
            k32_s.reshape(1, nb, 1, N_HEADS, HEAD_DIM), v32_s.reshape(1, nb, 1, N_HEADS, HEAD_DIM),
            small_s[:, :IDX_DIM].reshape(1, nb, 1, IDX_DIM), pool_sample_out[None])
```
